```python
import jax
import jax.numpy as jnp
from jax import lax

D_MODEL = 1024
BATCH = 2
SEQ = 8192
DEPTH = 2
DEC_BATCH = 32
DEC_SEQ = 4
PAST_LEN = 8192
PAGE_SIZE = 128

FOX_HEAD_DIM = 128
FOX_HEADS = D_MODEL // FOX_HEAD_DIM
FOX_WIDTH = FOX_HEADS * FOX_HEAD_DIM
Q_BLOCK = 128
N_MEM = 256
X_HEADS = 4
X_HEAD_DIM = 128
X_WIDTH = X_HEADS * X_HEAD_DIM
POOL_WINDOWS = (2, 4, 8, 16)
POOL_GROUPS = len(POOL_WINDOWS)
POOL_WIDTH = D_MODEL
POOL_GC = POOL_WIDTH // POOL_GROUPS
POOL_HIST = max(POOL_WINDOWS) - 1
BRANCH_WIDTH = FOX_WIDTH + X_WIDTH
FOX_IN = 3 * FOX_WIDTH + FOX_HEADS + X_WIDTH + BRANCH_WIDTH
POOL_IN = POOL_WIDTH + X_WIDTH + BRANCH_WIDTH
EPS = 1e-6

kernel_name = 'hybrid_fox_pool_memxattn_step'


def rmsnorm(x, g):
    xf = x.astype(jnp.float32)
    r = lax.rsqrt(jnp.mean(xf * xf, axis=-1, keepdims=True) + EPS)
    return (xf * r * g.astype(jnp.float32)).astype(x.dtype)


def mem_kv(mem, g, w):
    kv = rmsnorm(mem, g) @ w
    b = mem.shape[0]
    k = kv[..., :X_WIDTH].reshape(b, N_MEM, X_HEADS, X_HEAD_DIM)
    v = kv[..., X_WIDTH:].reshape(b, N_MEM, X_HEADS, X_HEAD_DIM)
    return k, v


def mem_attend(xq, mk, mv):
    b, t, _ = xq.shape
    q = xq.reshape(b, t, X_HEADS, X_HEAD_DIM)
    s = jnp.einsum('bqhd,bmhd->bhqm', q, mk).astype(jnp.float32) * (X_HEAD_DIM ** -0.5)
    p = jax.nn.softmax(s, axis=-1).astype(mv.dtype)
    return jnp.einsum('bhqm,bmhd->bqhd', p, mv).reshape(b, t, X_WIDTH)


def gated_out(mix, xo, gate, w):
    return (jnp.concatenate([mix, xo], axis=-1) * jax.nn.silu(gate)) @ w


def split_fox(z, b_f):
    lead = z.shape[:-1]
    hs = (FOX_HEADS, FOX_HEAD_DIM)
    q = z[..., :FOX_WIDTH].reshape(lead + hs)
    k = z[..., FOX_WIDTH:2 * FOX_WIDTH].reshape(lead + hs)
    v = z[..., 2 * FOX_WIDTH:3 * FOX_WIDTH].reshape(lead + hs)
    o = 3 * FOX_WIDTH
    f = z[..., o:o + FOX_HEADS]
    o += FOX_HEADS
    xq = z[..., o:o + X_WIDTH]
    gate = z[..., o + X_WIDTH:]
    lf = jax.nn.log_sigmoid((f + b_f).astype(jnp.float32))
    return q, k, v, lf, xq, gate


def split_pool(z):
    return z[..., :POOL_WIDTH], z[..., POOL_WIDTH:POOL_WIDTH + X_WIDTH], z[..., POOL_WIDTH + X_WIDTH:]


def fox_prompt(q, k, v, lf):
    b, t, h, hd = q.shape
    nblk = t // Q_BLOCK
    c = jnp.cumsum(lf, axis=1)
    c_keys = c.transpose(0, 2, 1)
    qb = q.reshape(b, nblk, Q_BLOCK, h, hd).transpose(1, 0, 2, 3, 4)
    cb = c.reshape(b, nblk, Q_BLOCK, h).transpose(1, 0, 3, 2)
    kpos = jnp.arange(t)
    scale = hd ** -0.5

    def one_block(args):
        i, q_i, c_i = args
        s = jnp.einsum('bqhd,bkhd->bhqk', q_i, k).astype(jnp.float32) * scale
        s = s + (c_i[..., :, None] - c_keys[:, :, None, :])
        qpos = i * Q_BLOCK + jnp.arange(Q_BLOCK)
        s = jnp.where(kpos[None, :] <= qpos[:, None], s, -jnp.inf)
        p = jax.nn.softmax(s, axis=-1).astype(v.dtype)
        return jnp.einsum('bhqk,bkhd->bqhd', p, v)

    out = lax.map(one_block, (jnp.arange(nblk), qb, cb))
    return out.transpose(1, 0, 2, 3, 4).reshape(b, t, h * hd)


def fox_sample(q, k_new, v_new, lf_new, k_past, v_past, lf_past):
    db, tn, h, hd = q.shape
    scale = hd ** -0.5
    c_new = jnp.cumsum(lf_new, axis=1).transpose(0, 2, 1)
    suf = jnp.flip(jnp.cumsum(jnp.flip(lf_past, 1), axis=1), 1) - lf_past
    suf = suf.transpose(0, 2, 1)
    s_past = jnp.einsum('bqhd,bkhd->bhqk', q, k_past).astype(jnp.float32) * scale
    s_past = s_past + suf[:, :, None, :] + c_new[:, :, :, None]
    s_new = jnp.einsum('bqhd,bkhd->bhqk', q, k_new).astype(jnp.float32) * scale
    s_new = s_new + (c_new[:, :, :, None] - c_new[:, :, None, :])
    tril = jnp.arange(tn)[None, :] <= jnp.arange(tn)[:, None]
    s_new = jnp.where(tril, s_new, -jnp.inf)
    p = jax.nn.softmax(jnp.concatenate([s_past, s_new], axis=-1), axis=-1).astype(v_new.dtype)
    n_past = k_past.shape[1]
    out = (jnp.einsum('bhqk,bkhd->bqhd', p[..., :n_past], v_past)
           + jnp.einsum('bhqk,bkhd->bqhd', p[..., n_past:], v_new))
    return out.reshape(db, tn, h * hd)


def multiscale_pool(u_ext, pos, w_group, scale):
    b, n, c = u_ext.shape
    t = n - POOL_HIST
    s = jnp.cumsum(u_ext.astype(jnp.float32), axis=1)
    s0 = jnp.concatenate([jnp.zeros((b, 1, c), jnp.float32), s], axis=1)
    end = s0[:, 1 + POOL_HIST:]
    outs = []
    for g, w in enumerate(POOL_WINDOWS):
        sl = slice(g * POOL_GC, (g + 1) * POOL_GC)
        start = s0[:, 1 + POOL_HIST - w:1 + POOL_HIST - w + t, sl]
        cnt = jnp.minimum(w, pos + 1).astype(jnp.float32)[None, :, None]
        outs.append((end[..., sl] - start) / cnt)
    pooled = jnp.concatenate(outs, axis=-1)
    d = (pooled - u_ext[:, POOL_HIST:].astype(jnp.float32)).astype(u_ext.dtype)
    y = jnp.einsum('btgc,gcd->btgd', d.reshape(b, t, POOL_GROUPS, POOL_GC), w_group)
    return y.reshape(b, t, c) * scale


def setup_inputs(seed: int = 0) -> dict:
    key = jax.random.key(seed)
    ks = jax.random.split(key, 20)
    f32 = jnp.float32
    n_pages = PAST_LEN // PAGE_SIZE
    n_used = DEC_BATCH * n_pages
    n_pool = n_used + (n_used + 3) // 4

    def nrm(k, shape, s=1.0):
        return s * jax.random.normal(k, shape, f32)

    page_table = jax.random.permutation(ks[9], n_pool)[:n_used].reshape(DEC_BATCH, n_pages).astype(jnp.int32)
    return {
        'x_prompt': nrm(ks[0], (BATCH, SEQ, D_MODEL)),
        'x_sample': nrm(ks[1], (DEC_BATCH, DEC_SEQ, D_MODEL)),
        'mem_prompt': nrm(ks[2], (BATCH, N_MEM, D_MODEL)),
        'cache_fox_k': nrm(ks[3], (n_pool, PAGE_SIZE, FOX_HEADS, FOX_HEAD_DIM)),
        'cache_fox_v': nrm(ks[4], (n_pool, PAGE_SIZE, FOX_HEADS, FOX_HEAD_DIM)),
        'cache_fox_logf': jax.nn.log_sigmoid(nrm(ks[5], (n_pool, PAGE_SIZE, FOX_HEADS)) + 4.0),
        'cache_mem_k': nrm(ks[6], (DEPTH, DEC_BATCH, N_MEM, X_HEADS, X_HEAD_DIM)),
        'cache_mem_v': nrm(ks[7], (DEPTH, DEC_BATCH, N_MEM, X_HEADS, X_HEAD_DIM)),
        'state_pool': nrm(ks[8], (DEC_BATCH, POOL_HIST, POOL_WIDTH)),
        'page_table': page_table,
        'norm_pre': 1.0 + nrm(ks[10], (DEPTH, D_MODEL), 0.1),
        'norm_post': 1.0 + nrm(ks[11], (DEPTH, D_MODEL), 0.1),
        'mem_norm': 1.0 + nrm(ks[12], (DEPTH, D_MODEL), 0.1),
        'w_mem_kv': nrm(ks[13], (DEPTH, D_MODEL, 2 * X_WIDTH), D_MODEL ** -0.5),
        'fox_w_in': nrm(ks[14], (D_MODEL, FOX_IN), D_MODEL ** -0.5),
        'fox_b_f': nrm(ks[15], (FOX_HEADS,), 0.01),
        'pool_w_in': nrm(ks[16], (D_MODEL, POOL_IN), D_MODEL ** -0.5),
        'pool_w_group': nrm(ks[17], (POOL_GROUPS, POOL_GC, POOL_GC), POOL_GC ** -0.5),
        'pool_scale': 1.0 + nrm(ks[18], (POOL_WIDTH,), 0.1),
        'w_out': nrm(ks[19], (DEPTH, BRANCH_WIDTH, D_MODEL), BRANCH_WIDTH ** -0.5),
    }


def reference(x_prompt, x_sample, mem_prompt, cache_fox_k, cache_fox_v, cache_fox_logf, cache_mem_k, cache_mem_v, state_pool, page_table, norm_pre, norm_post, mem_norm, w_mem_kv, fox_w_in, fox_b_f, pool_w_in, pool_w_group, pool_scale, w_out):
    b, t, _ = x_prompt.shape
    db, tn, _ = x_sample.shape
    n_past = page_table.shape[1] * PAGE_SIZE
    y_p, y_s = x_prompt, x_sample
    mem_k_list, mem_v_list = [], []
    for l in range(DEPTH):
        h_p = rmsnorm(y_p, norm_pre[l])
        h_s = rmsnorm(y_s, norm_pre[l])
        mk_p, mv_p = mem_kv(mem_prompt, mem_norm[l], w_mem_kv[l])
        mem_k_list.append(mk_p)
        mem_v_list.append(mv_p)
        mk_s, mv_s = cache_mem_k[l], cache_mem_v[l]
        if l % 2 == 0:
            q, k, v, lf, xq_p, g_p = split_fox(h_p @ fox_w_in, fox_b_f)
            mix_p = fox_prompt(q, k, v, lf)
            fox_k_p = k.reshape(b * t // PAGE_SIZE, PAGE_SIZE, FOX_HEADS, FOX_HEAD_DIM)
            fox_v_p = v.reshape(b * t // PAGE_SIZE, PAGE_SIZE, FOX_HEADS, FOX_HEAD_DIM)
            fox_lf_p = lf.reshape(b * t // PAGE_SIZE, PAGE_SIZE, FOX_HEADS)
            qs, ks_, vs, lfs, xq_s, g_s = split_fox(h_s @ fox_w_in, fox_b_f)
            k_past = cache_fox_k[page_table].reshape(db, n_past, FOX_HEADS, FOX_HEAD_DIM).astype(ks_.dtype)
            v_past = cache_fox_v[page_table].reshape(db, n_past, FOX_HEADS, FOX_HEAD_DIM).astype(vs.dtype)
            lf_past = cache_fox_logf[page_table].reshape(db, n_past, FOX_HEADS).astype(jnp.float32)
            mix_s = fox_sample(qs, ks_, vs, lfs, k_past, v_past, lf_past)
            fox_k_s, fox_v_s, fox_lf_s = ks_, vs, lfs
        else:
            u_p, xq_p, g_p = split_pool(h_p @ pool_w_in)
            u_ext_p = jnp.concatenate([jnp.zeros((b, POOL_HIST, POOL_WIDTH), u_p.dtype), u_p], axis=1)
            mix_p = multiscale_pool(u_ext_p, jnp.arange(t, dtype=jnp.int32), pool_w_group, pool_scale)
            pool_state_p = u_ext_p[:, -POOL_HIST:]
            u_s, xq_s, g_s = split_pool(h_s @ pool_w_in)
            u_ext_s = jnp.concatenate([state_pool.astype(u_s.dtype), u_s], axis=1)
            mix_s = multiscale_pool(u_ext_s, n_past + jnp.arange(tn, dtype=jnp.int32), pool_w_group, pool_scale)
            pool_state_s = u_ext_s[:, -POOL_HIST:]
        y_p = y_p + rmsnorm(gated_out(mix_p, mem_attend(xq_p, mk_p, mv_p), g_p, w_out[l]), norm_post[l])
        y_s = y_s + rmsnorm(gated_out(mix_s, mem_attend(xq_s, mk_s, mv_s), g_s, w_out[l]), norm_post[l])
    mem_k_p = jnp.stack(mem_k_list)
    mem_v_p = jnp.stack(mem_v_list)
    return (y_p, y_s, fox_k_p, fox_v_p, fox_lf_p, fox_k_s, fox_v_s, fox_lf_s, mem_k_p, mem_v_p, pool_state_p, pool_state_s)
```

```python
import functools

import jax
import jax.numpy as jnp
from jax import lax
from jax.experimental import pallas as pl
from jax.experimental.pallas import tpu as pltpu

F32 = jnp.float32
BF16 = jnp.bfloat16

D_MODEL = 1024
PAGE_SIZE = 128
FOX_HEADS = 8
FOX_HEAD_DIM = 128
FOX_WIDTH = FOX_HEADS * FOX_HEAD_DIM
N_MEM = 256
X_HEADS = 4
X_HEAD_DIM = 128
X_WIDTH = X_HEADS * X_HEAD_DIM
POOL_WINDOWS = (2, 4, 8, 16)
POOL_GC = D_MODEL // len(POOL_WINDOWS)
POOL_HIST = max(POOL_WINDOWS) - 1
HIST_ROWS = POOL_HIST + 1
BRANCH_WIDTH = FOX_WIDTH + X_WIDTH
EPS = 1e-6

LANES = 128
SUBLANES = 8
MASKED = -1e30
VMEM_LIMIT = 56 * 1024 * 1024

_NT = (((1,), (1,)), ((), ()))


def _rms(x, g):
    r = lax.rsqrt(jnp.mean(x * x, axis=-1, keepdims=True) + EPS)
    return x * r * g


def _dot(a, b):
    return jnp.dot(a, b, preferred_element_type=F32)


def _dot_nt(a, b):
    return lax.dot_general(a, b, _NT, preferred_element_type=F32)


def _silu(x):
    return x / (1.0 + jnp.exp(-x))


def _const_spec(shape):
    zeros = (0,) * len(shape)
    return pl.BlockSpec(shape, lambda *_: zeros, pipeline_mode=pl.Buffered(1))


def _memkv_kernel(mem_ref, g_ref, w_ref, out_ref):
    h = _rms(mem_ref[0], g_ref[0]).astype(BF16)
    out_ref[0, 0] = _dot(h, w_ref[0])


def _mem_kv(mem, mem_norm, w_bf16):
    depth = w_bf16.shape[0]
    b = mem.shape[0]
    return pl.pallas_call(
        _memkv_kernel,
        grid=(depth, b),
        in_specs=[
            pl.BlockSpec((1, N_MEM, D_MODEL), lambda l, i: (i, 0, 0)),
            pl.BlockSpec((1, 1, D_MODEL), lambda l, i: (l, 0, 0)),
            pl.BlockSpec((1, D_MODEL, 2 * X_WIDTH), lambda l, i: (l, 0, 0)),
        ],
        out_specs=pl.BlockSpec((1, 1, N_MEM, 2 * X_WIDTH), lambda l, i: (l, i, 0, 0)),
        out_shape=jax.ShapeDtypeStruct((depth, b, N_MEM, 2 * X_WIDTH), F32),
        compiler_params=pltpu.CompilerParams(dimension_semantics=("arbitrary", "arbitrary")),
        name="mem_kv",
    )(mem, mem_norm.reshape(depth, 1, D_MODEL), w_bf16)


def _cumsum_rows(x):
    n = x.shape[0]
    row = lax.broadcasted_iota(jnp.int32, x.shape, 0)
    k = 1
    while k < n:
        x = x + jnp.where(row >= k, pltpu.roll(x, k, axis=0), 0.0)
        k *= 2
    return x


def _inproj_kernel(x_ref, g_ref, wqkv_ref, wf_ref, bf_ref,
                   q_ref, k_ref, v_ref, kb_ref, vb_ref, lf_ref, c_ref, carry_ref,
                   *, tiles_per_seq, col_chunk):
    i = pl.program_id(0)
    h = _rms(x_ref[...], g_ref[...]).astype(BF16)
    n_chunks = FOX_WIDTH // col_chunk
    for c in range(n_chunks):
        cols = slice(c * col_chunk, (c + 1) * col_chunk)
        q_ref[:, cols] = _dot(h, wqkv_ref[:, cols]).astype(BF16)
    for c in range(n_chunks):
        cols = slice(c * col_chunk, (c + 1) * col_chunk)
        kk = _dot(h, wqkv_ref[:, FOX_WIDTH + c * col_chunk:FOX_WIDTH + (c + 1) * col_chunk])
        k_ref[:, cols] = kk
        kb_ref[:, cols] = kk.astype(BF16)
    for c in range(n_chunks):
        cols = slice(c * col_chunk, (c + 1) * col_chunk)
        vv = _dot(h, wqkv_ref[:, 2 * FOX_WIDTH + c * col_chunk:2 * FOX_WIDTH + (c + 1) * col_chunk])
        v_ref[:, cols] = vv
        vb_ref[:, cols] = vv.astype(BF16)

    z = _dot(h, wf_ref[...]) + bf_ref[...]
    lf = jnp.minimum(z, 0.0) - jnp.log1p(jnp.exp(-jnp.abs(z)))
    lf_ref[...] = lf[:, :FOX_HEADS]

    @pl.when(i % tiles_per_seq == 0)
    def _():
        carry_ref[...] = jnp.zeros_like(carry_ref)

    c_run = _cumsum_rows(lf) + carry_ref[...]
    c_ref[...] = c_run[:, :FOX_HEADS]
    carry_ref[...] = c_run[-1:, :]


def _fox_inproj(x2d, g, wqkv, wf, bf, *, tm, tiles_per_seq):
    n = x2d.shape[0]
    row_spec = lambda w: pl.BlockSpec((tm, w), lambda i: (i, 0))
    kern = functools.partial(_inproj_kernel, tiles_per_seq=tiles_per_seq, col_chunk=512)
    return pl.pallas_call(
        kern,
        grid=(n // tm,),
        in_specs=[
            row_spec(D_MODEL),
            _const_spec((1, D_MODEL)),
            _const_spec((D_MODEL, 3 * FOX_WIDTH)),
            _const_spec((D_MODEL, LANES)),
            _const_spec((1, LANES)),
        ],
        out_specs=[row_spec(FOX_WIDTH)] * 5 + [row_spec(FOX_HEADS)] * 2,
        out_shape=[
            jax.ShapeDtypeStruct((n, FOX_WIDTH), BF16),
            jax.ShapeDtypeStruct((n, FOX_WIDTH), F32),
            jax.ShapeDtypeStruct((n, FOX_WIDTH), F32),
            jax.ShapeDtypeStruct((n, FOX_WIDTH), BF16),
            jax.ShapeDtypeStruct((n, FOX_WIDTH), BF16),
            jax.ShapeDtypeStruct((n, FOX_HEADS), F32),
            jax.ShapeDtypeStruct((n, FOX_HEADS), F32),
        ],
        scratch_shapes=[pltpu.VMEM((1, LANES), F32)],
        compiler_params=pltpu.CompilerParams(
            dimension_semantics=("arbitrary",), vmem_limit_bytes=VMEM_LIMIT),
        name="fox_inproj",
    )(x2d, g, wqkv, wf, bf)


def _prefill_kernel(qi_ref, kj_ref, q_ref, k_ref, v_ref, cq_ref, ck_ref, o_ref,
                    m_ref, l_ref, acc_ref, *, tq, tk):
    p_idx = pl.program_id(1)
    qi = qi_ref[p_idx]
    kj = kj_ref[p_idx]

    @pl.when(kj == 0)
    def _():
        m_ref[...] = jnp.full_like(m_ref, MASKED)
        l_ref[...] = jnp.zeros_like(l_ref)
        acc_ref[...] = jnp.zeros_like(acc_ref)

    def update(masked):
        if masked:
            row = lax.broadcasted_iota(jnp.int32, (tq, tk), 0)
            col = lax.broadcasted_iota(jnp.int32, (tq, tk), 1)
            visible = col <= row
        for h in range(FOX_HEADS):
            cols = slice(h * FOX_HEAD_DIM, (h + 1) * FOX_HEAD_DIM)
            s = _dot_nt(q_ref[:, cols], k_ref[:, cols])
            bias = cq_ref[0, h:h + 1, 0:1] - ck_ref[0, h:h + 1, :]
            s = s + bias
            if masked:
                s = jnp.where(visible, s, MASKED)
            m_old = m_ref[h]
            m_new = jnp.maximum(m_old, jnp.max(s, axis=-1, keepdims=True))
            alpha = jnp.exp(m_old - m_new)
            p = jnp.exp(s - m_new)
            l_ref[h] = alpha * l_ref[h] + jnp.sum(p, axis=-1, keepdims=True)
            acc_ref[:, cols] = alpha * acc_ref[:, cols] + _dot(p.astype(BF16), v_ref[:, cols])
            m_ref[h] = m_new

    @pl.when(kj < qi)
    def _():
        update(False)

    @pl.when(kj == qi)
    def _():
        update(True)
        for h in range(FOX_HEADS):
            cols = slice(h * FOX_HEAD_DIM, (h + 1) * FOX_HEAD_DIM)
            o_ref[:, cols] = (acc_ref[:, cols] / l_ref[h]).astype(BF16)


def _fox_prefill(q, kb, vb, c_t, *, batch, seq, tile):
    nq = seq // tile
    pairs = [(i, j) for i in range(nq) for j in range(i + 1)]
    qi = jnp.asarray([p[0] for p in pairs], jnp.int32)
    kj = jnp.asarray([p[1] for p in pairs], jnp.int32)
    kern = functools.partial(_prefill_kernel, tq=tile, tk=tile)
    grid_spec = pltpu.PrefetchScalarGridSpec(
        num_scalar_prefetch=2,
        grid=(batch, len(pairs)),
        in_specs=[
            pl.BlockSpec((tile, FOX_WIDTH), lambda b, p, qi, kj: (b * nq + qi[p], 0)),
            pl.BlockSpec((tile, FOX_WIDTH), lambda b, p, qi, kj: (b * nq + kj[p], 0)),
            pl.BlockSpec((tile, FOX_WIDTH), lambda b, p, qi, kj: (b * nq + kj[p], 0)),
            pl.BlockSpec((1, FOX_HEADS, tile), lambda b, p, qi, kj: (b, 0, qi[p])),
            pl.BlockSpec((1, FOX_HEADS, tile), lambda b, p, qi, kj: (b, 0, kj[p])),
        ],
        out_specs=pl.BlockSpec((tile, FOX_WIDTH), lambda b, p, qi, kj: (b * nq + qi[p], 0)),
        scratch_shapes=[
            pltpu.VMEM((FOX_HEADS, tile, 1), F32),
            pltpu.VMEM((FOX_HEADS, tile, 1), F32),
            pltpu.VMEM((tile, FOX_WIDTH), F32),
        ],
    )
    return pl.pallas_call(
        kern,
        grid_spec=grid_spec,
        out_shape=jax.ShapeDtypeStruct((batch * seq, FOX_WIDTH), BF16),
        compiler_params=pltpu.CompilerParams(
            dimension_semantics=("arbitrary", "arbitrary"), vmem_limit_bytes=VMEM_LIMIT),
        name="fox_prefill",
    )(qi, kj, q, kb, vb, c_t, c_t)


def _suffix_sum_lanes(x):
    n = x.shape[-1]
    lane = lax.broadcasted_iota(jnp.int32, x.shape, 1)
    k = 1
    while k < n:
        x = x + jnp.where(lane < n - k, pltpu.roll(x, n - k, axis=1), 0.0)
        k *= 2
    return x


def _prefix_sum_lanes(x):
    n = x.shape[-1]
    lane = lax.broadcasted_iota(jnp.int32, x.shape, 1)
    k = 1
    while k < n:
        x = x + jnp.where(lane >= k, pltpu.roll(x, k, axis=1), 0.0)
        k *= 2
    return x


def _decode_kernel(pt_ref, q_ref, kn_ref, vn_ref, lfn_ref, *rest, pages_per_step, n_new):
    del pt_ref
    P = pages_per_step
    k_refs = rest[:P]
    v_refs = rest[P:2 * P]
    lf_refs = rest[2 * P:3 * P]
    o_ref, qblk_ref, m_ref, l_ref, acc_ref, carry_ref = rest[3 * P:]
    j = pl.program_id(1)
    rows = n_new * FOX_HEADS

    def online_update(s, v_bf16):
        m_old = m_ref[...]
        m_new = jnp.maximum(m_old, jnp.max(s, axis=-1, keepdims=True))
        alpha = jnp.exp(m_old - m_new)
        p = jnp.exp(s - m_new)
        l_ref[...] = alpha * l_ref[...] + jnp.sum(p, axis=-1, keepdims=True)
        acc_ref[...] = alpha * acc_ref[...] + _dot(p.astype(BF16), v_bf16)
        m_ref[...] = m_new

    @pl.when(j == 0)
    def _():
        head_of_col = lax.broadcasted_iota(jnp.int32, (FOX_HEADS, FOX_WIDTH), 1) // FOX_HEAD_DIM
        head_of_row = lax.broadcasted_iota(jnp.int32, (FOX_HEADS, FOX_WIDTH), 0)
        own = head_of_col == head_of_row
        q = q_ref[0].astype(F32)
        blks = [jnp.where(own, jnp.broadcast_to(q[t:t + 1, :], (FOX_HEADS, FOX_WIDTH)), 0.0)
                for t in range(n_new)]
        qblk_ref[...] = jnp.concatenate(blks, axis=0).astype(BF16)
        m_ref[...] = jnp.full_like(m_ref, MASKED)
        l_ref[...] = jnp.zeros_like(l_ref)
        acc_ref[...] = jnp.zeros_like(acc_ref)
        carry_ref[...] = jnp.zeros_like(carry_ref)

        pad = jnp.zeros((PAGE_SIZE - SUBLANES, FOX_WIDTH), F32)
        kn = jnp.concatenate([kn_ref[0], pad], axis=0).astype(BF16)
        vn = jnp.concatenate([vn_ref[0], pad], axis=0).astype(BF16)
        c_new = _prefix_sum_lanes(lfn_ref[0])
        s = _dot_nt(qblk_ref[...], kn) - jnp.concatenate([c_new] * n_new, axis=0)
        key = lax.broadcasted_iota(jnp.int32, (rows, PAGE_SIZE), 1)
        tok = lax.broadcasted_iota(jnp.int32, (rows, PAGE_SIZE), 0) // FOX_HEADS
        s = jnp.where(key <= tok, s, MASKED)
        online_update(s, vn)

    for i in range(P):
        lf = lf_refs[i][0]
        incl = _suffix_sum_lanes(lf)
        suf = incl - lf + carry_ref[...]
        carry_ref[...] = carry_ref[...] + incl[:, 0:1]
        s = _dot_nt(qblk_ref[...], k_refs[i][0].astype(BF16)) + jnp.concatenate([suf] * n_new, axis=0)
        online_update(s, v_refs[i][0].astype(BF16))

    @pl.when(j == pl.num_programs(1) - 1)
    def _():
        o = acc_ref[...] / l_ref[...]
        head_of_col = lax.broadcasted_iota(jnp.int32, (FOX_HEADS, FOX_WIDTH), 1) // FOX_HEAD_DIM
        head_of_row = lax.broadcasted_iota(jnp.int32, (FOX_HEADS, FOX_WIDTH), 0)
        own = head_of_col == head_of_row
        out_row = lax.broadcasted_iota(jnp.int32, (SUBLANES, FOX_WIDTH), 0)
        out = jnp.zeros((SUBLANES, FOX_WIDTH), F32)
        for t in range(n_new):
            picked = jnp.sum(jnp.where(own, o[t * FOX_HEADS:(t + 1) * FOX_HEADS, :], 0.0),
                             axis=0, keepdims=True)
            out = jnp.where(out_row == t, jnp.broadcast_to(picked, out.shape), out)
        o_ref[0] = out.astype(BF16)


def _fox_decode(page_table, q_pad, k_new_pad, v_new_pad, lf_new_t, cache_k, cache_v, cache_lf_t,
                *, n_new, pages_per_step):
    db, n_pages = page_table.shape
    P = pages_per_step
    n_steps = n_pages // P

    def page_map(i):
        return lambda b, j, pt: (pt[b, n_pages - 1 - (j * P + i)], 0, 0)

    tok_spec = pl.BlockSpec((1, SUBLANES, FOX_WIDTH), lambda b, j, pt: (b, 0, 0))
    in_specs = [tok_spec, tok_spec, tok_spec,
                pl.BlockSpec((1, FOX_HEADS, LANES), lambda b, j, pt: (b, 0, 0))]
    in_specs += [pl.BlockSpec((1, PAGE_SIZE, FOX_WIDTH), page_map(i)) for i in range(P)]
    in_specs += [pl.BlockSpec((1, PAGE_SIZE, FOX_WIDTH), page_map(i)) for i in range(P)]
    in_specs += [pl.BlockSpec((1, FOX_HEADS, PAGE_SIZE), page_map(i)) for i in range(P)]
    rows = n_new * FOX_HEADS
    grid_spec = pltpu.PrefetchScalarGridSpec(
        num_scalar_prefetch=1,
        grid=(db, n_steps),
        in_specs=in_specs,
        out_specs=pl.BlockSpec((1, SUBLANES, FOX_WIDTH), lambda b, j, pt: (b, 0, 0)),
        scratch_shapes=[
            pltpu.VMEM((rows, FOX_WIDTH), BF16),
            pltpu.VMEM((rows, 1), F32),
            pltpu.VMEM((rows, 1), F32),
            pltpu.VMEM((rows, FOX_WIDTH), F32),
            pltpu.VMEM((FOX_HEADS, LANES), F32),
        ],
    )
    kern = functools.partial(_decode_kernel, pages_per_step=P, n_new=n_new)
    return pl.pallas_call(
        kern,
        grid_spec=grid_spec,
        out_shape=jax.ShapeDtypeStruct((db, SUBLANES, FOX_WIDTH), BF16),
        compiler_params=pltpu.CompilerParams(
            dimension_semantics=("arbitrary", "arbitrary"), vmem_limit_bytes=VMEM_LIMIT),
        name="fox_decode",
    )(page_table, q_pad, k_new_pad, v_new_pad, lf_new_t,
      *([cache_k] * P), *([cache_v] * P), *([cache_lf_t] * P))


def _mem_attend(xq, mk_ref, mv_ref):
    xq = xq.astype(BF16)
    outs = []
    for h in range(X_HEADS):
        cols = slice(h * X_HEAD_DIM, (h + 1) * X_HEAD_DIM)
        s = _dot_nt(xq[:, cols], mk_ref[0, 0, :, cols].astype(BF16))
        p = jnp.exp(s - jnp.max(s, axis=-1, keepdims=True))
        p = p / jnp.sum(p, axis=-1, keepdims=True)
        outs.append(_dot(p.astype(BF16), mv_ref[0, 0, :, cols].astype(BF16)))
    return outs


def _tail_kernel(x_ref, mix_ref, mk0_ref, mv0_ref, mk1_ref, mv1_ref, hist_ref,
                 gpre_ref, gpost_ref, wxg_ref, wout0_ref, wp_ref, wg_ref, pscale_ref, wout1_ref,
                 y_ref, state_ref, ubuf_ref, *, tm, n_valid, pos0, n_tiles):
    t = pl.program_id(1)

    x = x_ref[...]
    h = _rms(x, gpre_ref[0:1, :]).astype(BF16)
    xq = _dot(h, wxg_ref[:, :X_WIDTH])
    xo = _mem_attend(xq, mk0_ref, mv0_ref)
    o = jnp.zeros((tm, D_MODEL), F32)
    for c in range(FOX_WIDTH // 512):
        cols = slice(c * 512, (c + 1) * 512)
        gate = _silu(_dot(h, wxg_ref[:, X_WIDTH + c * 512:X_WIDTH + (c + 1) * 512]))
        o = o + _dot((mix_ref[:, cols].astype(F32) * gate).astype(BF16), wout0_ref[cols, :])
    gate = _silu(_dot(h, wxg_ref[:, X_WIDTH + FOX_WIDTH:]))
    for hh in range(X_HEADS):
        cols = slice(hh * X_HEAD_DIM, (hh + 1) * X_HEAD_DIM)
        o = o + _dot((xo[hh] * gate[:, cols]).astype(BF16),
                     wout0_ref[FOX_WIDTH + hh * X_HEAD_DIM:FOX_WIDTH + (hh + 1) * X_HEAD_DIM, :])
    y1 = x + _rms(o, gpost_ref[0:1, :])

    h = _rms(y1, gpre_ref[1:2, :]).astype(BF16)

    @pl.when(t == 0)
    def _():
        ubuf_ref[0:HIST_ROWS, :] = hist_ref[0]

    for c in range(D_MODEL // 512):
        cols = slice(c * 512, (c + 1) * 512)
        ubuf_ref[HIST_ROWS:, cols] = _dot(h, wp_ref[:, cols])

    pos = pos0 + t * tm + lax.broadcasted_iota(jnp.int32, (tm, 1), 0)
    o = jnp.zeros((tm, D_MODEL), F32)
    for g, w in enumerate(POOL_WINDOWS):
        cols = slice(g * POOL_GC, (g + 1) * POOL_GC)
        ext = ubuf_ref[:, cols]
        acc = ext
        k = 1
        while k < w:
            acc = acc + pltpu.roll(acc, k, axis=0)
            k *= 2
        cnt = jnp.minimum(w, pos + 1).astype(F32)
        u = ext[HIST_ROWS:, :]
        d = (acc[HIST_ROWS:, :] / cnt - u).astype(BF16)
        mix1 = _dot(d, wg_ref[g]) * pscale_ref[:, cols]
        gate = _silu(_dot(h, wp_ref[:, D_MODEL + X_WIDTH + g * POOL_GC:D_MODEL + X_WIDTH + (g + 1) * POOL_GC]))
        o = o + _dot((mix1 * gate).astype(BF16), wout1_ref[cols, :])

    state_ref[0] = ubuf_ref[n_valid:n_valid + HIST_ROWS, :]
    if n_tiles > 1:
        ubuf_ref[0:HIST_ROWS, :] = ubuf_ref[tm:tm + HIST_ROWS, :]

    xq = _dot(h, wp_ref[:, D_MODEL:D_MODEL + X_WIDTH])
    xo = _mem_attend(xq, mk1_ref, mv1_ref)
    gate = _silu(_dot(h, wp_ref[:, D_MODEL + X_WIDTH + D_MODEL:]))
    for hh in range(X_HEADS):
        cols = slice(hh * X_HEAD_DIM, (hh + 1) * X_HEAD_DIM)
        o = o + _dot((xo[hh] * gate[:, cols]).astype(BF16),
                     wout1_ref[D_MODEL + hh * X_HEAD_DIM:D_MODEL + (hh + 1) * X_HEAD_DIM, :])
    y_ref[...] = y1 + _rms(o, gpost_ref[1:2, :])


def _trunk_tail(x2d, mix, mem_k, mem_v, hist, norm_pre, norm_post, wxg, wout0, wp, wg, pscale, wout1,
                *, groups, tm, n_valid, pos0):
    n = x2d.shape[0]
    n_tiles = n // (groups * tm)
    row_spec = pl.BlockSpec((tm, D_MODEL), lambda g, t: (g * n_tiles + t, 0))
    mem_spec = lambda l: pl.BlockSpec((1, 1, N_MEM, X_WIDTH), lambda g, t: (l, g, 0, 0))
    state_spec = pl.BlockSpec((1, HIST_ROWS, D_MODEL), lambda g, t: (g, 0, 0))
    kern = functools.partial(_tail_kernel, tm=tm, n_valid=n_valid, pos0=pos0, n_tiles=n_tiles)
    return pl.pallas_call(
        kern,
        grid=(groups, n_tiles),
        in_specs=[
            row_spec, row_spec, mem_spec(0), mem_spec(0), mem_spec(1), mem_spec(1), state_spec,
            _const_spec(norm_pre.shape), _const_spec(norm_post.shape),
            _const_spec(wxg.shape), _const_spec(wout0.shape), _const_spec(wp.shape),
            _const_spec(wg.shape), _const_spec(pscale.shape), _const_spec(wout1.shape),
        ],
        out_specs=[row_spec, state_spec],
        out_shape=[
            jax.ShapeDtypeStruct((n, D_MODEL), F32),
            jax.ShapeDtypeStruct((groups, HIST_ROWS, D_MODEL), F32),
        ],
        scratch_shapes=[pltpu.VMEM((HIST_ROWS + tm, D_MODEL), F32)],
        compiler_params=pltpu.CompilerParams(
            dimension_semantics=("arbitrary", "arbitrary"), vmem_limit_bytes=VMEM_LIMIT),
        name="trunk_tail",
    )(x2d, mix, mem_k, mem_v, mem_k, mem_v, hist, norm_pre, norm_post, wxg, wout0, wp, wg, pscale, wout1)


def _pick_tile(n, want):
    t = min(n, want)
    while n % t:
        t //= 2
    return t


def kernel(x_prompt, x_sample, mem_prompt, cache_fox_k, cache_fox_v, cache_fox_logf, cache_mem_k,
           cache_mem_v, state_pool, page_table, norm_pre, norm_post, mem_norm, w_mem_kv, fox_w_in,
           fox_b_f, pool_w_in, pool_w_group, pool_scale, w_out):
    b, t, _ = x_prompt.shape
    db, tn, _ = x_sample.shape
    depth = norm_pre.shape[0]
    n_pool = cache_fox_k.shape[0]
    n_pages = page_table.shape[1]
    n_past = n_pages * PAGE_SIZE
    assert depth == 2 and tn <= SUBLANES and t % PAGE_SIZE == 0

    fox_scale = FOX_HEAD_DIM ** -0.5
    x_scale = X_HEAD_DIM ** -0.5
    o_f = 3 * FOX_WIDTH
    o_xq = o_f + FOX_HEADS
    o_gate = o_xq + X_WIDTH
    wqkv = jnp.concatenate([fox_w_in[:, :FOX_WIDTH] * fox_scale, fox_w_in[:, FOX_WIDTH:o_f]], axis=1).astype(BF16)
    wf = jnp.pad(fox_w_in[:, o_f:o_xq], ((0, 0), (0, LANES - FOX_HEADS))).astype(BF16)
    bf = jnp.pad(fox_b_f, (0, LANES - FOX_HEADS)).reshape(1, LANES)
    wxg = jnp.concatenate([fox_w_in[:, o_xq:o_gate] * x_scale, fox_w_in[:, o_gate:]], axis=1).astype(BF16)
    wp = jnp.concatenate([pool_w_in[:, :D_MODEL], pool_w_in[:, D_MODEL:D_MODEL + X_WIDTH] * x_scale,
                          pool_w_in[:, D_MODEL + X_WIDTH:]], axis=1).astype(BF16)
    wout = w_out.astype(BF16)
    wg = pool_w_group.astype(BF16)
    pscale = pool_scale.reshape(1, D_MODEL)
    g_pre0 = norm_pre[0:1]

    kv = _mem_kv(mem_prompt, mem_norm, w_mem_kv.astype(BF16))
    mem_k_p = kv[..., :X_WIDTH]
    mem_v_p = kv[..., X_WIDTH:]

    xp = x_prompt.reshape(b * t, D_MODEL)
    tm_in = _pick_tile(t, 512)
    q, k, v, kb, vb, lf, c = _fox_inproj(xp, g_pre0, wqkv, wf, bf, tm=tm_in, tiles_per_seq=t // tm_in)
    c_t = c.reshape(b, t, FOX_HEADS).transpose(0, 2, 1)
    mix_p = _fox_prefill(q, kb, vb, c_t, batch=b, seq=t, tile=_pick_tile(t, 512))
    tm_tail = _pick_tile(t, 256)
    y_p, state_p = _trunk_tail(
        xp, mix_p, mem_k_p, mem_v_p, jnp.zeros((b, HIST_ROWS, D_MODEL), F32),
        norm_pre, norm_post, wxg, wout[0], wp, wg, pscale, wout[1],
        groups=b, tm=tm_tail, n_valid=tm_tail, pos0=0)

    pad_rows = SUBLANES - tn
    xs = jnp.pad(x_sample, ((0, 0), (0, pad_rows), (0, 0))).reshape(db * SUBLANES, D_MODEL)
    n_s = db * SUBLANES
    q_s, k_s, v_s, _, _, lf_s, _ = _fox_inproj(xs, g_pre0, wqkv, wf, bf, tm=n_s, tiles_per_seq=1)
    valid = (lax.broadcasted_iota(jnp.int32, (1, SUBLANES, 1), 1) < tn)
    k_s3 = jnp.where(valid, k_s.reshape(db, SUBLANES, FOX_WIDTH), 0.0)
    v_s3 = jnp.where(valid, v_s.reshape(db, SUBLANES, FOX_WIDTH), 0.0)
    lf_s3 = lf_s.reshape(db, SUBLANES, FOX_HEADS)[:, :tn]
    lf_new_t = jnp.pad(lf_s3.transpose(0, 2, 1), ((0, 0), (0, 0), (0, LANES - tn)))
    mix_s = _fox_decode(
        page_table, q_s.reshape(db, SUBLANES, FOX_WIDTH), k_s3, v_s3, lf_new_t,
        cache_fox_k.reshape(n_pool, PAGE_SIZE, FOX_WIDTH), cache_fox_v.reshape(n_pool, PAGE_SIZE, FOX_WIDTH),
        cache_fox_logf.transpose(0, 2, 1), n_new=tn, pages_per_step=_pick_tile(n_pages, 8))
    hist_s = jnp.pad(state_pool, ((0, 0), (HIST_ROWS - POOL_HIST, 0), (0, 0)))
    y_s, state_s = _trunk_tail(
        xs, mix_s.reshape(n_s, FOX_WIDTH),
        cache_mem_k.reshape(depth, db, N_MEM, X_WIDTH), cache_mem_v.reshape(depth, db, N_MEM, X_WIDTH),
        hist_s, norm_pre, norm_post, wxg, wout[0], wp, wg, pscale, wout[1],
        groups=db, tm=SUBLANES, n_valid=tn, pos0=n_past)

    hd = (FOX_HEADS, FOX_HEAD_DIM)
    xd = (X_HEADS, X_HEAD_DIM)
    return (
        y_p.reshape(b, t, D_MODEL),
        y_s.reshape(db, SUBLANES, D_MODEL)[:, :tn],
        k.reshape((b * t // PAGE_SIZE, PAGE_SIZE) + hd),
        v.reshape((b * t // PAGE_SIZE, PAGE_SIZE) + hd),
        lf.reshape(b * t // PAGE_SIZE, PAGE_SIZE, FOX_HEADS),
        k_s.reshape(db, SUBLANES, FOX_WIDTH)[:, :tn].reshape((db, tn) + hd),
        v_s.reshape(db, SUBLANES, FOX_WIDTH)[:, :tn].reshape((db, tn) + hd),
        lf_s3,
        mem_k_p.reshape((depth, b, N_MEM) + xd),
        mem_v_p.reshape((depth, b, N_MEM) + xd),
        state_p[:, HIST_ROWS - POOL_HIST:],
        state_s[:, HIST_ROWS - POOL_HIST:],
    )
```

```python
import functools

import jax
import jax.numpy as jnp
from jax import lax
from jax.experimental import pallas as pl
from jax.experimental.pallas import tpu as pltpu

F32 = jnp.float32
BF16 = jnp.bfloat16

D_MODEL = 1024
PAGE_SIZE = 128
FOX_HEADS = 8
FOX_HEAD_DIM = 128
FOX_WIDTH = FOX_HEADS * FOX_HEAD_DIM
N_MEM = 256
X_HEADS = 4
X_HEAD_DIM = 128
X_WIDTH = X_HEADS * X_HEAD_DIM
POOL_WINDOWS = (2, 4, 8, 16)
POOL_GC = D_MODEL // len(POOL_WINDOWS)
POOL_HIST = max(POOL_WINDOWS) - 1
HIST_ROWS = POOL_HIST + 1
BRANCH_WIDTH = FOX_WIDTH + X_WIDTH
EPS = 1e-6

LANES = 128
SUBLANES = 8
MASKED = -1e30
VMEM_LIMIT = 56 * 1024 * 1024
LOG2E = 1.4426950408889634
ROW_BLOCK = 256
COL_BLOCK = 128

_NT = (((1,), (1,)), ((), ()))


def _rms(x, g):
    r = lax.rsqrt(jnp.mean(x * x, axis=-1, keepdims=True) + EPS)
    return x * r * g


def _dot(a, b):
    return jnp.dot(a, b, preferred_element_type=F32)


def _dot_nt(a, b):
    return lax.dot_general(a, b, _NT, preferred_element_type=F32)


def _silu(x):
    return x / (1.0 + jnp.exp(-x))


def _const_spec(shape):
    zeros = (0,) * len(shape)
    return pl.BlockSpec(shape, lambda *_: zeros, pipeline_mode=pl.Buffered(1))


def _memkv_kernel(mem_ref, g_ref, w_ref, out_ref):
    h = _rms(mem_ref[0], g_ref[0]).astype(BF16)
    out_ref[0, 0] = _dot(h, w_ref[0])


def _mem_kv(mem, mem_norm, w_bf16):
    depth = w_bf16.shape[0]
    b = mem.shape[0]
    return pl.pallas_call(
        _memkv_kernel,
        grid=(depth, b),
        in_specs=[
            pl.BlockSpec((1, N_MEM, D_MODEL), lambda l, i: (i, 0, 0)),
            pl.BlockSpec((1, 1, D_MODEL), lambda l, i: (l, 0, 0)),
            pl.BlockSpec((1, D_MODEL, 2 * X_WIDTH), lambda l, i: (l, 0, 0)),
        ],
        out_specs=pl.BlockSpec((1, 1, N_MEM, 2 * X_WIDTH), lambda l, i: (l, i, 0, 0)),
        out_shape=jax.ShapeDtypeStruct((depth, b, N_MEM, 2 * X_WIDTH), F32),
        compiler_params=pltpu.CompilerParams(dimension_semantics=("arbitrary", "arbitrary")),
        name="mem_kv",
    )(mem, mem_norm.reshape(depth, 1, D_MODEL), w_bf16)


def _cumsum_rows(x):
    n = x.shape[0]
    row = lax.broadcasted_iota(jnp.int32, x.shape, 0)
    k = 1
    while k < n:
        x = x + jnp.where(row >= k, pltpu.roll(x, k, axis=0), 0.0)
        k *= 2
    return x


def _inproj_kernel(x_ref, g_ref, wqkv_ref, wf_ref, bf_ref,
                   q_ref, k_ref, v_ref, kb_ref, vb_ref, lf_ref, c_ref, carry_ref,
                   *, tiles_per_seq, col_chunk, v_transposed):
    i = pl.program_id(0)
    tm = x_ref.shape[0]
    h = _rms(x_ref[...], g_ref[...]).astype(BF16)
    n_chunks = FOX_WIDTH // col_chunk
    heads_per_chunk = col_chunk // FOX_HEAD_DIM
    for c in range(n_chunks):
        cols = slice(c * col_chunk, (c + 1) * col_chunk)
        q_ref[:, cols] = _dot(h, wqkv_ref[:, cols]).astype(BF16)
    for base, f32_ref, bf16_ref, transposed in ((FOX_WIDTH, k_ref, kb_ref, False),
                                                (2 * FOX_WIDTH, v_ref, vb_ref, v_transposed)):
        for c in range(n_chunks):
            cols = slice(c * col_chunk, (c + 1) * col_chunk)
            kk = _dot(h, wqkv_ref[:, base + c * col_chunk:base + (c + 1) * col_chunk])
            if not transposed:
                bf16_ref[:, cols] = kk.astype(BF16)
            for hh in range(heads_per_chunk):
                head = c * heads_per_chunk + hh
                kh = kk[:, hh * FOX_HEAD_DIM:(hh + 1) * FOX_HEAD_DIM]
                f32_ref[pl.ds(head, tm, stride=FOX_HEADS), :] = kh
                if transposed:
                    bf16_ref[head] = kh.T.astype(BF16)

    z = _dot(h, wf_ref[...]) + bf_ref[...]
    lf = jnp.minimum(z, 0.0) - jnp.log1p(jnp.exp(-jnp.abs(z)))
    lf_ref[...] = lf[:, :FOX_HEADS]

    @pl.when(i % tiles_per_seq == 0)
    def _():
        carry_ref[...] = jnp.zeros_like(carry_ref)

    c_run = _cumsum_rows(lf) + carry_ref[...]
    c_ref[...] = c_run[:, :FOX_HEADS]
    carry_ref[...] = c_run[-1:, :]


def _fox_inproj(x2d, g, wqkv, wf, bf, *, tm, tiles_per_seq, v_transposed):
    n = x2d.shape[0]
    row_spec = lambda w: pl.BlockSpec((tm, w), lambda i: (i, 0))
    cache_spec = pl.BlockSpec((tm * FOX_HEADS, FOX_HEAD_DIM), lambda i: (i, 0))
    cache_shape = jax.ShapeDtypeStruct((n * FOX_HEADS, FOX_HEAD_DIM), F32)
    if v_transposed:
        vb_spec = pl.BlockSpec((FOX_HEADS, FOX_HEAD_DIM, tm), lambda i: (0, 0, i))
        vb_shape = jax.ShapeDtypeStruct((FOX_HEADS, FOX_HEAD_DIM, n), BF16)
    else:
        vb_spec = row_spec(FOX_WIDTH)
        vb_shape = jax.ShapeDtypeStruct((n, FOX_WIDTH), BF16)
    kern = functools.partial(_inproj_kernel, tiles_per_seq=tiles_per_seq, col_chunk=512,
                             v_transposed=v_transposed)
    return pl.pallas_call(
        kern,
        grid=(n // tm,),
        in_specs=[
            row_spec(D_MODEL),
            _const_spec((1, D_MODEL)),
            _const_spec((D_MODEL, 3 * FOX_WIDTH)),
            _const_spec((D_MODEL, LANES)),
            _const_spec((1, LANES)),
        ],
        out_specs=[row_spec(FOX_WIDTH), cache_spec, cache_spec, row_spec(FOX_WIDTH), vb_spec]
        + [row_spec(FOX_HEADS)] * 2,
        out_shape=[
            jax.ShapeDtypeStruct((n, FOX_WIDTH), BF16),
            cache_shape,
            cache_shape,
            jax.ShapeDtypeStruct((n, FOX_WIDTH), BF16),
            vb_shape,
            jax.ShapeDtypeStruct((n, FOX_HEADS), F32),
            jax.ShapeDtypeStruct((n, FOX_HEADS), F32),
        ],
        scratch_shapes=[pltpu.VMEM((1, LANES), F32)],
        compiler_params=pltpu.CompilerParams(
            dimension_semantics=("arbitrary",), vmem_limit_bytes=VMEM_LIMIT),
        name="fox_inproj",
    )(x2d, g, wqkv, wf, bf)


def _prefill_kernel(qi_ref, kj_ref, q_ref, k_ref, vt_ref, cq_ref, ck_ref, o_ref,
                    m_ref, l_ref, acc_ref, bias_ref, *, tq, tk):
    p_idx = pl.program_id(1)
    qi = qi_ref[p_idx]
    kj = kj_ref[p_idx]

    @pl.when(kj == 0)
    def _():
        m_ref[...] = jnp.full_like(m_ref, MASKED)
        l_ref[...] = jnp.zeros_like(l_ref)
        acc_ref[...] = jnp.zeros_like(acc_ref)

    for h in range(FOX_HEADS):
        rel = (cq_ref[0:1, h:h + 1] - ck_ref[:, h:h + 1]) * LOG2E
        bias_ref[h] = jnp.broadcast_to(rel, (tk, LANES))

    def head_block(h, rb, r0, key_blocks):
        cols = slice(h * FOX_HEAD_DIM, (h + 1) * FOX_HEAD_DIM)
        qh = q_ref[pl.ds(r0, ROW_BLOCK), cols]
        m = m_ref[h, rb]
        l = l_ref[h, rb]
        acc = acc_ref[h, rb]
        for cb, masked in key_blocks:
            keys = slice(cb * COL_BLOCK, (cb + 1) * COL_BLOCK)
            bias = jnp.concatenate([bias_ref[h, keys, :]] * (ROW_BLOCK // LANES), axis=1)
            s = _dot_nt(k_ref[keys, cols], qh) + bias
            if masked:
                key_id = cb * COL_BLOCK + lax.broadcasted_iota(jnp.int32, (COL_BLOCK, ROW_BLOCK), 0)
                q_id = r0 + lax.broadcasted_iota(jnp.int32, (COL_BLOCK, ROW_BLOCK), 1)
                s = jnp.where(key_id <= q_id, s, MASKED)
            m_next = jnp.maximum(m, jnp.max(s, axis=0, keepdims=True))
            alpha = jnp.exp2(m - m_next)
            p = jnp.exp2(s - m_next)
            l = alpha * l + jnp.sum(p, axis=0, keepdims=True)
            acc = alpha * acc + _dot(vt_ref[h, :, keys], p.astype(BF16))
            m = m_next
        m_ref[h, rb] = m
        l_ref[h, rb] = l
        acc_ref[h, rb] = acc

    n_rb = tq // ROW_BLOCK
    n_cb = tk // COL_BLOCK

    @pl.when(kj < qi)
    def _():
        def body(rb, carry):
            r0 = pl.multiple_of(rb * ROW_BLOCK, ROW_BLOCK)
            for h in range(FOX_HEADS):
                head_block(h, rb, r0, [(cb, False) for cb in range(n_cb)])
            return carry

        lax.fori_loop(0, n_rb, body, 0)

    @pl.when(kj == qi)
    def _():
        for rb in range(n_rb):
            r0 = rb * ROW_BLOCK
            blocks = []
            for cb in range(n_cb):
                c0 = cb * COL_BLOCK
                if c0 <= r0 + ROW_BLOCK - 1:
                    blocks.append((cb, c0 + COL_BLOCK - 1 > r0))
            for h in range(FOX_HEADS):
                head_block(h, rb, r0, blocks)
                out_t = acc_ref[h, rb] / l_ref[h, rb]
                o_ref[r0:r0 + ROW_BLOCK, h * FOX_HEAD_DIM:(h + 1) * FOX_HEAD_DIM] = out_t.T.astype(BF16)


def _fox_prefill(q, kb, vt, c, *, batch, seq, tile):
    assert tile % COL_BLOCK == 0 and tile % ROW_BLOCK == 0
    nq = seq // tile
    n_rb = tile // ROW_BLOCK
    pairs = [(i, j) for i in range(nq) for j in range(i + 1)]
    qi = jnp.asarray([p[0] for p in pairs], jnp.int32)
    kj = jnp.asarray([p[1] for p in pairs], jnp.int32)
    kern = functools.partial(_prefill_kernel, tq=tile, tk=tile)
    q_map = lambda b, p, qi, kj: (b * nq + qi[p], 0)
    k_map = lambda b, p, qi, kj: (b * nq + kj[p], 0)
    grid_spec = pltpu.PrefetchScalarGridSpec(
        num_scalar_prefetch=2,
        grid=(batch, len(pairs)),
        in_specs=[
            pl.BlockSpec((tile, FOX_WIDTH), q_map),
            pl.BlockSpec((tile, FOX_WIDTH), k_map),
            pl.BlockSpec((FOX_HEADS, FOX_HEAD_DIM, tile), lambda b, p, qi, kj: (0, 0, b * nq + kj[p])),
            pl.BlockSpec((tile, FOX_HEADS), q_map),
            pl.BlockSpec((tile, FOX_HEADS), k_map),
        ],
        out_specs=pl.BlockSpec((tile, FOX_WIDTH), q_map),
        scratch_shapes=[
            pltpu.VMEM((FOX_HEADS, n_rb, 1, ROW_BLOCK), F32),
            pltpu.VMEM((FOX_HEADS, n_rb, 1, ROW_BLOCK), F32),
            pltpu.VMEM((FOX_HEADS, n_rb, FOX_HEAD_DIM, ROW_BLOCK), F32),
            pltpu.VMEM((FOX_HEADS, tile, LANES), F32),
        ],
    )
    return pl.pallas_call(
        kern,
        grid_spec=grid_spec,
        out_shape=jax.ShapeDtypeStruct((batch * seq, FOX_WIDTH), BF16),
        compiler_params=pltpu.CompilerParams(
            dimension_semantics=("arbitrary", "arbitrary"), vmem_limit_bytes=VMEM_LIMIT),
        name="fox_prefill",
    )(qi, kj, q, kb, vt, c, c)


def _suffix_sum_lanes(x, stride):
    n = x.shape[-1]
    lane = lax.broadcasted_iota(jnp.int32, x.shape, 1)
    k = stride
    while k < n:
        x = x + jnp.where(lane < n - k, pltpu.roll(x, n - k, axis=1), 0.0)
        k *= 2
    return x


def _prefix_sum_lanes(x, stride):
    n = x.shape[-1]
    lane = lax.broadcasted_iota(jnp.int32, x.shape, 1)
    k = stride
    while k < n:
        x = x + jnp.where(lane >= k, pltpu.roll(x, k, axis=1), 0.0)
        k *= 2
    return x


def _decode_kernel(pt_ref, q_ref, kn_ref, vn_ref, lfn_ref, *rest, pages_per_step, n_new):
    del pt_ref
    P = pages_per_step
    k_refs = rest[:P]
    v_refs = rest[P:2 * P]
    lf_refs = rest[2 * P:3 * P]
    o_ref, q32_ref, m_ref, l_ref, acc_ref, carry_ref = rest[3 * P:]
    j = pl.program_id(1)
    rows = n_new * FOX_HEADS
    page_lanes = PAGE_SIZE * FOX_HEADS

    def own_head(n_lanes):
        row = lax.broadcasted_iota(jnp.int32, (rows, n_lanes), 0)
        lane = lax.broadcasted_iota(jnp.int32, (rows, n_lanes), 1)
        return (lane & (FOX_HEADS - 1)) == (row & (FOX_HEADS - 1)), row, lane

    def online_update(scores, values):
        m_prev = m_ref[...]
        s_max = scores[0]
        for s in scores[1:]:
            s_max = jnp.maximum(s_max, s)
        m_next = jnp.maximum(m_prev, jnp.max(s_max, axis=1, keepdims=True))
        alpha = jnp.exp2(m_prev - m_next)
        m_tiled = jnp.concatenate([m_next] * (s_max.shape[1] // LANES), axis=1)
        acc = alpha * acc_ref[...]
        p_sum = jnp.zeros_like(s_max)
        for s, v in zip(scores, values):
            p = jnp.exp2(s - m_tiled)
            p_sum = p_sum + p
            acc = acc + _dot(p.astype(BF16), v)
        l_ref[...] = alpha * l_ref[...] + jnp.sum(p_sum, axis=1, keepdims=True)
        acc_ref[...] = acc
        m_ref[...] = m_next

    @pl.when(j == 0)
    def _():
        q = q_ref[0].astype(F32)
        for t in range(n_new):
            for h in range(FOX_HEADS):
                r = t * FOX_HEADS + h
                q32_ref[r:r + 1, :] = q[t:t + 1, h * FOX_HEAD_DIM:(h + 1) * FOX_HEAD_DIM]
        m_ref[...] = jnp.full_like(m_ref, MASKED)
        l_ref[...] = jnp.zeros_like(l_ref)
        acc_ref[...] = jnp.zeros_like(acc_ref)
        carry_ref[...] = jnp.zeros_like(carry_ref)

        pad = jnp.zeros((LANES - kn_ref.shape[0], FOX_HEAD_DIM), F32)
        kn = jnp.concatenate([kn_ref[...], pad], axis=0).astype(BF16)
        vn = jnp.concatenate([vn_ref[...], pad], axis=0).astype(BF16)
        c_new = _prefix_sum_lanes(lfn_ref[0], FOX_HEADS) * LOG2E
        own, row, lane = own_head(LANES)
        key_tok = lane // FOX_HEADS
        visible = own & (key_tok <= row // FOX_HEADS) & (key_tok < n_new)
        s = _dot_nt(q32_ref[...].astype(BF16), kn) - c_new
        online_update([jnp.where(visible, s, MASKED)], [vn])

    own, _, _ = own_head(page_lanes)
    q32 = q32_ref[...].astype(BF16)
    carry = carry_ref[...]
    scores, values = [], []
    for i in range(P):
        lf = lf_refs[i][0]
        after = _suffix_sum_lanes(lf, FOX_HEADS)
        upto = _prefix_sum_lanes(lf, FOX_HEADS)
        bias = (after - lf + carry) * LOG2E
        carry = carry + (after + upto - lf)
        scores.append(_dot_nt(q32, k_refs[i][0].astype(BF16)) + jnp.where(own, bias, MASKED))
        values.append(v_refs[i][0].astype(BF16))
    carry_ref[...] = carry
    online_update(scores, values)

    @pl.when(j == pl.num_programs(1) - 1)
    def _():
        o = acc_ref[...] / l_ref[...]
        o_ref[0] = jnp.zeros(o_ref.shape[1:], F32)
        for t in range(n_new):
            for h in range(FOX_HEADS):
                r = t * FOX_HEADS + h
                o_ref[0, t:t + 1, h * FOX_HEAD_DIM:(h + 1) * FOX_HEAD_DIM] = o[r:r + 1, :]


def _fox_decode(page_table, q_pad, k_new, v_new, lf_new, cache_k, cache_v, cache_lf,
                *, n_new, pages_per_step):
    db, n_pages = page_table.shape
    P = pages_per_step
    n_steps = n_pages // P
    rows = n_new * FOX_HEADS
    page_rows = PAGE_SIZE * FOX_HEADS
    slot_rows = SUBLANES * FOX_HEADS

    def page_map(i):
        return lambda b, j, pt: (pt[b, n_pages - 1 - (j * P + i)], 0, 0)

    seq_map = lambda b, j, pt: (b, 0, 0)
    new_spec = pl.BlockSpec((slot_rows, FOX_HEAD_DIM), lambda b, j, pt: (b, 0))
    in_specs = [pl.BlockSpec((1, SUBLANES, FOX_WIDTH), seq_map), new_spec, new_spec,
                pl.BlockSpec((1, 1, LANES), seq_map)]
    in_specs += [pl.BlockSpec((1, page_rows, FOX_HEAD_DIM), page_map(i)) for i in range(P)]
    in_specs += [pl.BlockSpec((1, page_rows, FOX_HEAD_DIM), page_map(i)) for i in range(P)]
    in_specs += [pl.BlockSpec((1, 1, page_rows), page_map(i)) for i in range(P)]
    grid_spec = pltpu.PrefetchScalarGridSpec(
        num_scalar_prefetch=1,
        grid=(db, n_steps),
        in_specs=in_specs,
        out_specs=pl.BlockSpec((1, SUBLANES, FOX_WIDTH), seq_map),
        scratch_shapes=[
            pltpu.VMEM((rows, FOX_HEAD_DIM), F32),
            pltpu.VMEM((rows, LANES), F32),
            pltpu.VMEM((rows, LANES), F32),
            pltpu.VMEM((rows, FOX_HEAD_DIM), F32),
            pltpu.VMEM((1, page_rows), F32),
        ],
    )
    kern = functools.partial(_decode_kernel, pages_per_step=P, n_new=n_new)
    return pl.pallas_call(
        kern,
        grid_spec=grid_spec,
        out_shape=jax.ShapeDtypeStruct((db, SUBLANES, FOX_WIDTH), F32),
        compiler_params=pltpu.CompilerParams(
            dimension_semantics=("arbitrary", "arbitrary"), vmem_limit_bytes=VMEM_LIMIT),
        name="fox_decode",
    )(page_table, q_pad, k_new, v_new, lf_new,
      *([cache_k] * P), *([cache_v] * P), *([cache_lf] * P))


def _mem_attend(xq, mk_ref, mv_ref):
    xq = xq.astype(BF16)
    outs = []
    for h in range(X_HEADS):
        cols = slice(h * X_HEAD_DIM, (h + 1) * X_HEAD_DIM)
        s = _dot_nt(xq[:, cols], mk_ref[0, 0, :, cols].astype(BF16))
        p = jnp.exp(s - jnp.max(s, axis=-1, keepdims=True))
        p = p / jnp.sum(p, axis=-1, keepdims=True)
        outs.append(_dot(p.astype(BF16), mv_ref[0, 0, :, cols].astype(BF16)))
    return outs


def _tail_kernel(x_ref, mix_ref, mk0_ref, mv0_ref, mk1_ref, mv1_ref, hist_ref,
                 gpre_ref, gpost_ref, wxg_ref, wout0_ref, wp_ref, wg_ref, pscale_ref, wout1_ref,
                 y_ref, state_ref, ubuf_ref, *, tm, n_valid, pos0, n_tiles):
    t = pl.program_id(1)

    x = x_ref[...]
    h = _rms(x, gpre_ref[0:1, :]).astype(BF16)
    xq = _dot(h, wxg_ref[:, :X_WIDTH])
    xo = _mem_attend(xq, mk0_ref, mv0_ref)
    o = jnp.zeros((tm, D_MODEL), F32)
    for c in range(FOX_WIDTH // 512):
        cols = slice(c * 512, (c + 1) * 512)
        gate = _silu(_dot(h, wxg_ref[:, X_WIDTH + c * 512:X_WIDTH + (c + 1) * 512]))
        o = o + _dot((mix_ref[:, cols].astype(F32) * gate).astype(BF16), wout0_ref[cols, :])
    gate = _silu(_dot(h, wxg_ref[:, X_WIDTH + FOX_WIDTH:]))
    for hh in range(X_HEADS):
        cols = slice(hh * X_HEAD_DIM, (hh + 1) * X_HEAD_DIM)
        o = o + _dot((xo[hh] * gate[:, cols]).astype(BF16),
                     wout0_ref[FOX_WIDTH + hh * X_HEAD_DIM:FOX_WIDTH + (hh + 1) * X_HEAD_DIM, :])
    y1 = x + _rms(o, gpost_ref[0:1, :])

    h = _rms(y1, gpre_ref[1:2, :]).astype(BF16)

    @pl.when(t == 0)
    def _():
        ubuf_ref[0:HIST_ROWS, :] = hist_ref[0]

    for c in range(D_MODEL // 512):
        cols = slice(c * 512, (c + 1) * 512)
        ubuf_ref[HIST_ROWS:, cols] = _dot(h, wp_ref[:, cols])

    pos = pos0 + t * tm + lax.broadcasted_iota(jnp.int32, (tm, 1), 0)
    o = jnp.zeros((tm, D_MODEL), F32)
    for g, w in enumerate(POOL_WINDOWS):
        cols = slice(g * POOL_GC, (g + 1) * POOL_GC)
        ext = ubuf_ref[:, cols]
        acc = ext
        k = 1
        while k < w:
            acc = acc + pltpu.roll(acc, k, axis=0)
            k *= 2
        cnt = jnp.minimum(w, pos + 1).astype(F32)
        u = ext[HIST_ROWS:, :]
        d = (acc[HIST_ROWS:, :] / cnt - u).astype(BF16)
        mix1 = _dot(d, wg_ref[g]) * pscale_ref[:, cols]
        gate = _silu(_dot(h, wp_ref[:, D_MODEL + X_WIDTH + g * POOL_GC:D_MODEL + X_WIDTH + (g + 1) * POOL_GC]))
        o = o + _dot((mix1 * gate).astype(BF16), wout1_ref[cols, :])

    state_ref[0] = ubuf_ref[n_valid:n_valid + HIST_ROWS, :]
    if n_tiles > 1:
        ubuf_ref[0:HIST_ROWS, :] = ubuf_ref[tm:tm + HIST_ROWS, :]

    xq = _dot(h, wp_ref[:, D_MODEL:D_MODEL + X_WIDTH])
    xo = _mem_attend(xq, mk1_ref, mv1_ref)
    gate = _silu(_dot(h, wp_ref[:, D_MODEL + X_WIDTH + D_MODEL:]))
    for hh in range(X_HEADS):
        cols = slice(hh * X_HEAD_DIM, (hh + 1) * X_HEAD_DIM)
        o = o + _dot((xo[hh] * gate[:, cols]).astype(BF16),
                     wout1_ref[D_MODEL + hh * X_HEAD_DIM:D_MODEL + (hh + 1) * X_HEAD_DIM, :])
    y_ref[...] = y1 + _rms(o, gpost_ref[1:2, :])


def _trunk_tail(x2d, mix, mem_k, mem_v, hist, norm_pre, norm_post, wxg, wout0, wp, wg, pscale, wout1,
                *, groups, tm, n_valid, pos0):
    n = x2d.shape[0]
    n_tiles = n // (groups * tm)
    row_spec = pl.BlockSpec((tm, D_MODEL), lambda g, t: (g * n_tiles + t, 0))
    mem_spec = lambda l: pl.BlockSpec((1, 1, N_MEM, X_WIDTH), lambda g, t: (l, g, 0, 0))
    state_spec = pl.BlockSpec((1, HIST_ROWS, D_MODEL), lambda g, t: (g, 0, 0))
    kern = functools.partial(_tail_kernel, tm=tm, n_valid=n_valid, pos0=pos0, n_tiles=n_tiles)
    return pl.pallas_call(
        kern,
        grid=(groups, n_tiles),
        in_specs=[
            row_spec, row_spec, mem_spec(0), mem_spec(0), mem_spec(1), mem_spec(1), state_spec,
            _const_spec(norm_pre.shape), _const_spec(norm_post.shape),
            _const_spec(wxg.shape), _const_spec(wout0.shape), _const_spec(wp.shape),
            _const_spec(wg.shape), _const_spec(pscale.shape), _const_spec(wout1.shape),
        ],
        out_specs=[row_spec, state_spec],
        out_shape=[
            jax.ShapeDtypeStruct((n, D_MODEL), F32),
            jax.ShapeDtypeStruct((groups, HIST_ROWS, D_MODEL), F32),
        ],
        scratch_shapes=[pltpu.VMEM((HIST_ROWS + tm, D_MODEL), F32)],
        compiler_params=pltpu.CompilerParams(
            dimension_semantics=("arbitrary", "arbitrary"), vmem_limit_bytes=VMEM_LIMIT),
        name="trunk_tail",
    )(x2d, mix, mem_k, mem_v, mem_k, mem_v, hist, norm_pre, norm_post, wxg, wout0, wp, wg, pscale, wout1)


def _pick_tile(n, want):
    t = min(n, want)
    while n % t:
        t //= 2
    return t


def kernel(x_prompt, x_sample, mem_prompt, cache_fox_k, cache_fox_v, cache_fox_logf, cache_mem_k,
           cache_mem_v, state_pool, page_table, norm_pre, norm_post, mem_norm, w_mem_kv, fox_w_in,
           fox_b_f, pool_w_in, pool_w_group, pool_scale, w_out):
    b, t, _ = x_prompt.shape
    db, tn, _ = x_sample.shape
    depth = norm_pre.shape[0]
    n_pool = cache_fox_k.shape[0]
    n_pages = page_table.shape[1]
    n_past = n_pages * PAGE_SIZE
    assert depth == 2 and tn <= SUBLANES and t % PAGE_SIZE == 0

    fox_scale = FOX_HEAD_DIM ** -0.5
    x_scale = X_HEAD_DIM ** -0.5
    o_f = 3 * FOX_WIDTH
    o_xq = o_f + FOX_HEADS
    o_gate = o_xq + X_WIDTH
    wqkv = jnp.concatenate([fox_w_in[:, :FOX_WIDTH] * (fox_scale * LOG2E), fox_w_in[:, FOX_WIDTH:o_f]],
                           axis=1).astype(BF16)
    wf = jnp.pad(fox_w_in[:, o_f:o_xq], ((0, 0), (0, LANES - FOX_HEADS))).astype(BF16)
    bf = jnp.pad(fox_b_f, (0, LANES - FOX_HEADS)).reshape(1, LANES)
    wxg = jnp.concatenate([fox_w_in[:, o_xq:o_gate] * x_scale, fox_w_in[:, o_gate:]], axis=1).astype(BF16)
    wp = jnp.concatenate([pool_w_in[:, :D_MODEL], pool_w_in[:, D_MODEL:D_MODEL + X_WIDTH] * x_scale,
                          pool_w_in[:, D_MODEL + X_WIDTH:]], axis=1).astype(BF16)
    wout = w_out.astype(BF16)
    wg = pool_w_group.astype(BF16)
    pscale = pool_scale.reshape(1, D_MODEL)
    g_pre0 = norm_pre[0:1]

    kv = _mem_kv(mem_prompt, mem_norm, w_mem_kv.astype(BF16))
    mem_k_p = kv[..., :X_WIDTH]
    mem_v_p = kv[..., X_WIDTH:]

    xp = x_prompt.reshape(b * t, D_MODEL)
    tm_in = _pick_tile(t, 512)
    q, k, v, kb, vt, lf, c = _fox_inproj(xp, g_pre0, wqkv, wf, bf, tm=tm_in, tiles_per_seq=t // tm_in,
                                         v_transposed=True)
    mix_p = _fox_prefill(q, kb, vt, c, batch=b, seq=t, tile=_pick_tile(t, 512))
    tm_tail = _pick_tile(t, 256)
    y_p, state_p = _trunk_tail(
        xp, mix_p, mem_k_p, mem_v_p, jnp.zeros((b, HIST_ROWS, D_MODEL), F32),
        norm_pre, norm_post, wxg, wout[0], wp, wg, pscale, wout[1],
        groups=b, tm=tm_tail, n_valid=tm_tail, pos0=0)

    pad_rows = SUBLANES - tn
    xs = jnp.pad(x_sample, ((0, 0), (0, pad_rows), (0, 0))).reshape(db * SUBLANES, D_MODEL)
    n_s = db * SUBLANES
    q_s, k_s, v_s, _, _, lf_s, _ = _fox_inproj(xs, g_pre0, wqkv, wf, bf, tm=n_s, tiles_per_seq=1,
                                               v_transposed=False)
    lf_s3 = lf_s.reshape(db, SUBLANES, FOX_HEADS)[:, :tn]
    lf_new = jnp.pad(lf_s.reshape(db, 1, SUBLANES * FOX_HEADS),
                     ((0, 0), (0, 0), (0, LANES - SUBLANES * FOX_HEADS)))
    page_rows = (n_pool, PAGE_SIZE * FOX_HEADS, FOX_HEAD_DIM)
    mix_s = _fox_decode(
        page_table, q_s.reshape(db, SUBLANES, FOX_WIDTH), k_s, v_s, lf_new,
        cache_fox_k.reshape(page_rows), cache_fox_v.reshape(page_rows),
        cache_fox_logf.reshape(n_pool, 1, PAGE_SIZE * FOX_HEADS),
        n_new=tn, pages_per_step=_pick_tile(n_pages, 8))
    hist_s = jnp.pad(state_pool, ((0, 0), (HIST_ROWS - POOL_HIST, 0), (0, 0)))
    y_s, state_s = _trunk_tail(
        xs, mix_s.reshape(n_s, FOX_WIDTH),
        cache_mem_k.reshape(depth, db, N_MEM, X_WIDTH), cache_mem_v.reshape(depth, db, N_MEM, X_WIDTH),
        hist_s, norm_pre, norm_post, wxg, wout[0], wp, wg, pscale, wout[1],
        groups=db, tm=SUBLANES, n_valid=tn, pos0=n_past)

    hd = (FOX_HEADS, FOX_HEAD_DIM)
    xd = (X_HEADS, X_HEAD_DIM)
    return (
        y_p.reshape(b, t, D_MODEL),
        y_s.reshape(db, SUBLANES, D_MODEL)[:, :tn],
        k.reshape((b * t // PAGE_SIZE, PAGE_SIZE) + hd),
        v.reshape((b * t // PAGE_SIZE, PAGE_SIZE) + hd),
        lf.reshape(b * t // PAGE_SIZE, PAGE_SIZE, FOX_HEADS),
        k_s.reshape((db, SUBLANES) + hd)[:, :tn],
        v_s.reshape((db, SUBLANES) + hd)[:, :tn],
        lf_s3,
        mem_k_p.reshape((depth, b, N_MEM) + xd),
        mem_v_p.reshape((depth, b, N_MEM) + xd),
        state_p[:, HIST_ROWS - POOL_HIST:],
        state_s[:, HIST_ROWS - POOL_HIST:],
    )
```
